```python
import jax, jax.numpy as jnp
from jax import lax
import numpy as np

D_MODEL = 1024
BATCH = 8
SEQ = 4096
DEPTH = 4

N_A_LAYERS = DEPTH // 2
N_B_LAYERS = DEPTH - N_A_LAYERS
EPS = 1e-6

CHUNK = 128
A_WIDTH = 2 * D_MODEL
A_GROUP_WIDTH = 256
A_GROUPS = A_WIDTH // A_GROUP_WIDTH

HEAD_DIM = 128
DILATED_GROUPS = ((128, 1), (512, 4), (2048, 16))
N_GROUPS = len(DILATED_GROUPS)
Q_HEADS_PER_GROUP = D_MODEL // HEAD_DIM
KV_HEADS_PER_GROUP = 2
Q_PER_KV = Q_HEADS_PER_GROUP // KV_HEADS_PER_GROUP
N_Q_HEADS = N_GROUPS * Q_HEADS_PER_GROUP
N_KV_HEADS = N_GROUPS * KV_HEADS_PER_GROUP
B_WIDTH = Q_HEADS_PER_GROUP * HEAD_DIM
BAND = 128

kernel_name = "yoco_gmlp_dilated_hybrid"


def rms_norm(x, g):
    xf = x.astype(jnp.float32)
    y = xf * lax.rsqrt(jnp.mean(xf * xf, axis=-1, keepdims=True) + EPS)
    return (y * g.astype(jnp.float32)).astype(x.dtype)


def ada_mod(c, w, b, n):
    h = jax.nn.silu(c) @ w + b
    return jnp.split(h[:, None, :], n, axis=-1)


def alibi_slopes():
    h = jnp.arange(1, N_Q_HEADS + 1, dtype=jnp.float32)
    return jnp.exp2(-8.0 * h / N_Q_HEADS)


def to_dilated(t, d):
    b, s = t.shape[:2]
    rest = t.shape[2:]
    span = d * BAND
    sp = -(-s // span) * span
    t = jnp.pad(t, [(0, 0), (0, sp - s)] + [(0, 0)] * len(rest))
    t = t.reshape((b, sp // d, d) + rest)
    t = jnp.swapaxes(t, 1, 2)
    return t.reshape((b, d, sp // span, BAND) + rest)


def from_dilated(t, s):
    b, d, nb = t.shape[:3]
    rest = t.shape[4:]
    t = t.reshape((b, d, nb * BAND) + rest)
    t = jnp.swapaxes(t, 1, 2)
    return t.reshape((b, nb * BAND * d) + rest)[:, :s]


def band_keys(t):
    prev = jnp.pad(t[:, :, :-1], [(0, 0), (0, 0), (1, 0)] + [(0, 0)] * 3)
    return jnp.concatenate([prev, t], axis=3)


def gmlp_layer(x, c, ada_w, ada_b, norm_g, w_in, sgu_g, w_s, b_s, w_out):
    bsz, s, _ = x.shape
    shift, scale, gate = ada_mod(c, ada_w, ada_b, 3)
    h = rms_norm(x, norm_g) * (1 + scale) + shift
    u, v, z = jnp.split(h @ w_in, 3, axis=-1)
    u = jax.nn.gelu(u)
    v = rms_norm(jax.nn.gelu(v), sgu_g)
    v = v.reshape(bsz, s // CHUNK, CHUNK, A_GROUPS, A_GROUP_WIDTH)
    w_causal = jnp.tril(w_s)
    mixed = jnp.einsum('gts,bcsge->bctge', w_causal, v) + b_s.T[None, None, :, :, None]
    y = u * mixed.reshape(bsz, s, A_WIDTH) * jax.nn.silu(z)
    return x + gate * (y @ w_out)


def shared_kv(x, c, ada_w, ada_b, norm_g, w_kv, k_norm_g):
    bsz, s, _ = x.shape
    shift, scale = ada_mod(c, ada_w, ada_b, 2)
    h = rms_norm(x, norm_g) * (1 + scale) + shift
    k, v = jnp.split(h @ w_kv, 2, axis=-1)
    k = rms_norm(k.reshape(bsz, s, N_KV_HEADS, HEAD_DIM), k_norm_g)
    v = v.reshape(bsz, s, N_KV_HEADS, HEAD_DIM)
    groups = []
    for g, (window, d) in enumerate(DILATED_GROUPS):
        hs = slice(g * KV_HEADS_PER_GROUP, (g + 1) * KV_HEADS_PER_GROUP)
        groups.append((band_keys(to_dilated(k[:, :, hs], d)),
                       band_keys(to_dilated(v[:, :, hs], d))))
    return groups


def dilated_group_attn(q, kb, vb, slopes, window, d):
    s = q.shape[1]
    ql = to_dilated(q, d)
    bsz, _, nb = ql.shape[:3]
    ql = ql.reshape(bsz, d, nb, BAND, KV_HEADS_PER_GROUP, Q_PER_KV, HEAD_DIM)
    sc = jnp.einsum('brnikgh,brnjkh->brnkgij', ql, kb, preferred_element_type=jnp.float32)
    i = jnp.arange(BAND)[:, None]
    j = jnp.arange(2 * BAND)[None, :]
    dq = BAND + i - j
    key_idx = (jnp.arange(nb)[:, None, None] - 1) * BAND + j
    valid = (dq >= 0) & (dq <= window // d) & (key_idx >= 0)
    bias = -slopes.astype(jnp.float32).reshape(KV_HEADS_PER_GROUP, Q_PER_KV)[:, :, None, None] \
        * (d * dq).astype(jnp.float32)
    sc = jnp.where(valid[:, None, None], sc + bias, -jnp.inf)
    m = jnp.max(sc, axis=-1, keepdims=True)
    p = jnp.exp(sc - m)
    l = jnp.sum(p, axis=-1, keepdims=True)
    o = jnp.einsum('brnkgij,brnjkh->brnikgh', p / l, vb.astype(jnp.float32))
    lse = jnp.moveaxis((m + jnp.log(l))[..., 0], -1, 3)
    o = from_dilated(o.reshape(bsz, d, nb, BAND, Q_HEADS_PER_GROUP, HEAD_DIM), s)
    lse = from_dilated(lse.reshape(bsz, d, nb, BAND, Q_HEADS_PER_GROUP), s)
    return o, lse


def dilated_layer(x, c, kv_groups, ada_w, ada_b, norm_g, w_in, q_norm_g, w_out):
    bsz, s, _ = x.shape
    shift, scale, gate = ada_mod(c, ada_w, ada_b, 3)
    h = rms_norm(x, norm_g) * (1 + scale) + shift
    qz = h @ w_in
    q = qz[..., :N_Q_HEADS * HEAD_DIM].reshape(bsz, s, N_Q_HEADS, HEAD_DIM)
    z = qz[..., N_Q_HEADS * HEAD_DIM:]
    q = rms_norm(q, q_norm_g) * (HEAD_DIM ** -0.5)
    slopes = alibi_slopes()
    outs, lses = [], []
    for g, (window, d) in enumerate(DILATED_GROUPS):
        hs = slice(g * Q_HEADS_PER_GROUP, (g + 1) * Q_HEADS_PER_GROUP)
        kb, vb = kv_groups[g]
        o, lse = dilated_group_attn(q[:, :, hs], kb, vb, slopes[hs], window, d)
        outs.append(o)
        lses.append(lse)
    alpha = jax.nn.softmax(jnp.stack(lses), axis=0)
    o = jnp.sum(alpha[..., None] * jnp.stack(outs), axis=0).astype(x.dtype)
    y = o.reshape(bsz, s, B_WIDTH) * jax.nn.silu(z)
    return x + gate * (y @ w_out)


def setup_inputs(seed: int = 0) -> dict:
    key = jax.random.key(seed)
    ks = jax.random.split(key, 24)
    nrm = lambda k, shape, sc: jax.random.normal(k, shape, jnp.float32) * sc
    D = D_MODEL
    return {
        "x": nrm(ks[0], (BATCH, SEQ, D), 1.0),
        "c": nrm(ks[1], (BATCH, D), 1.0),
        "a_ada_w": nrm(ks[2], (N_A_LAYERS, D, 3 * D), 0.5 * D ** -0.5),
        "a_ada_b": nrm(ks[3], (N_A_LAYERS, 3 * D), 0.01),
        "a_norm_g": 1.0 + nrm(ks[4], (N_A_LAYERS, D), 0.05),
        "a_w_in": nrm(ks[5], (N_A_LAYERS, D, 3 * A_WIDTH), D ** -0.5),
        "a_sgu_g": 1.0 + nrm(ks[6], (N_A_LAYERS, A_WIDTH), 0.05),
        "a_w_spatial": nrm(ks[7], (N_A_LAYERS, A_GROUPS, CHUNK, CHUNK), 0.5 * CHUNK ** -0.5),
        "a_b_spatial": 1.0 + nrm(ks[8], (N_A_LAYERS, A_GROUPS, CHUNK), 0.01),
        "a_w_out": nrm(ks[9], (N_A_LAYERS, A_WIDTH, D), A_WIDTH ** -0.5),
        "kv_ada_w": nrm(ks[10], (D, 2 * D), 0.5 * D ** -0.5),
        "kv_ada_b": nrm(ks[11], (2 * D,), 0.01),
        "kv_norm_g": 1.0 + nrm(ks[12], (D,), 0.05),
        "w_kv": nrm(ks[13], (D, 2 * N_KV_HEADS * HEAD_DIM), D ** -0.5),
        "k_norm_g": 1.0 + nrm(ks[14], (HEAD_DIM,), 0.05),
        "b_ada_w": nrm(ks[15], (N_B_LAYERS, D, 3 * D), 0.5 * D ** -0.5),
        "b_ada_b": nrm(ks[16], (N_B_LAYERS, 3 * D), 0.01),
        "b_norm_g": 1.0 + nrm(ks[17], (N_B_LAYERS, D), 0.05),
        "b_w_in": nrm(ks[18], (N_B_LAYERS, D, N_Q_HEADS * HEAD_DIM + B_WIDTH), D ** -0.5),
        "b_q_norm_g": 1.0 + nrm(ks[19], (N_B_LAYERS, HEAD_DIM), 0.05),
        "b_w_out": nrm(ks[20], (N_B_LAYERS, B_WIDTH, D), B_WIDTH ** -0.5),
    }


def reference(x, c, a_ada_w, a_ada_b, a_norm_g, a_w_in, a_sgu_g, a_w_spatial, a_b_spatial,
              a_w_out, kv_ada_w, kv_ada_b, kv_norm_g, w_kv, k_norm_g, b_ada_w, b_ada_b,
              b_norm_g, b_w_in, b_q_norm_g, b_w_out):
    kv_groups = None
    for layer in range(DEPTH):
        if layer < N_A_LAYERS:
            x = gmlp_layer(x, c, a_ada_w[layer], a_ada_b[layer], a_norm_g[layer], a_w_in[layer],
                           a_sgu_g[layer], a_w_spatial[layer], a_b_spatial[layer], a_w_out[layer])
        else:
            if layer == N_A_LAYERS:
                kv_groups = shared_kv(x, c, kv_ada_w, kv_ada_b, kv_norm_g, w_kv, k_norm_g)
            lb = layer - N_A_LAYERS
            x = dilated_layer(x, c, kv_groups, b_ada_w[lb], b_ada_b[lb], b_norm_g[lb], b_w_in[lb],
                              b_q_norm_g[lb], b_w_out[lb])
    return x
```

```python
import functools
import math

import jax
import jax.numpy as jnp
from jax import lax
from jax.experimental import pallas as pl
from jax.experimental.pallas import tpu as pltpu

D_MODEL = 1024
DEPTH = 4
N_A_LAYERS = DEPTH // 2
N_B_LAYERS = DEPTH - N_A_LAYERS
EPS = 1e-6

CHUNK = 128
A_WIDTH = 2 * D_MODEL
A_GROUP_WIDTH = 256
A_GROUPS = A_WIDTH // A_GROUP_WIDTH

HEAD_DIM = 128
DILATED_GROUPS = ((128, 1), (512, 4), (2048, 16))
N_GROUPS = len(DILATED_GROUPS)
Q_HEADS_PER_GROUP = D_MODEL // HEAD_DIM
KV_HEADS_PER_GROUP = 2
Q_PER_KV = Q_HEADS_PER_GROUP // KV_HEADS_PER_GROUP
N_Q_HEADS = N_GROUPS * Q_HEADS_PER_GROUP
N_KV_HEADS = N_GROUPS * KV_HEADS_PER_GROUP
B_WIDTH = Q_HEADS_PER_GROUP * HEAD_DIM
BAND = 128
Q_WIDTH = N_Q_HEADS * HEAD_DIM
KV_WIDTH = N_KV_HEADS * HEAD_DIM
KV_GROUP_WIDTH = KV_HEADS_PER_GROUP * HEAD_DIM

V7X_VMEM_BYTES = 64 * 1024 * 1024
VMEM_LIMIT_BYTES = V7X_VMEM_BYTES - 8 * 1024 * 1024
LANES = 128

TOKEN_TILE = 512
MASKED = -1e30

F32 = jnp.float32
BF16 = jnp.bfloat16


def _params():
    return pltpu.CompilerParams(vmem_limit_bytes=VMEM_LIMIT_BYTES)


def _resident(shape):
    return pl.BlockSpec(shape, lambda *_: (0,) * len(shape), pipeline_mode=pl.Buffered(1))


def _gelu_tanh(x):
    return 0.5 * x * (1.0 + jnp.tanh(math.sqrt(2.0 / math.pi) * (x + 0.044715 * (x * x * x))))


def _silu(x):
    return x / (1.0 + jnp.exp(-x))


def _rms(x, g):
    return x * lax.rsqrt(jnp.mean(x * x, axis=-1, keepdims=True) + EPS) * g


def _mod_norm(x, g, scale, shift):
    return _rms(x, g) * (1.0 + scale) + shift


def _dot(a, b):
    return jnp.dot(a, b, preferred_element_type=F32)


def _ada_kernel(c_ref, w_ref, b_ref, o_ref):
    a = _silu(c_ref[...]).astype(BF16)
    o_ref[0] = _dot(a, w_ref[0].astype(BF16)) + b_ref[0]


def _ada(c, w, b):
    n_layers, d, n = w.shape
    bsz = c.shape[0]
    bn = 1024
    return pl.pallas_call(
        _ada_kernel,
        grid=(n_layers, n // bn),
        in_specs=[
            pl.BlockSpec((bsz, d), lambda l, j: (0, 0)),
            pl.BlockSpec((1, d, bn), lambda l, j: (l, 0, j)),
            pl.BlockSpec((1, 1, bn), lambda l, j: (l, 0, j)),
        ],
        out_specs=pl.BlockSpec((1, bsz, bn), lambda l, j: (l, 0, j)),
        out_shape=jax.ShapeDtypeStruct((n_layers, bsz, n), F32),
        compiler_params=_params(),
        name="ada",
    )(c, w, b.reshape(n_layers, 1, n))


def _gmlp_kernel(x_ref, mod_ref, ng_ref, win_ref, sg_ref, ws_ref, bs_ref, wout_ref, o_ref,
                 gv_ref, y_ref):
    x = x_ref[0]
    mod = mod_ref[0]
    h = _mod_norm(x, ng_ref[...], mod[1:2], mod[0:1]).astype(BF16)
    tile = x.shape[0]

    ssq = jnp.zeros((tile, 1), F32)
    for g in range(A_GROUPS):
        cols = slice(g * A_GROUP_WIDTH, (g + 1) * A_GROUP_WIDTH)
        gv = _gelu_tanh(_dot(h, win_ref[:, A_WIDTH + g * A_GROUP_WIDTH:A_WIDTH + (g + 1) * A_GROUP_WIDTH]))
        ssq = ssq + jnp.sum(gv * gv, axis=-1, keepdims=True)
        gv_ref[:, cols] = gv
    inv = lax.rsqrt(ssq * (1.0 / A_WIDTH) + EPS)

    t_idx = lax.broadcasted_iota(jnp.int32, (CHUNK, CHUNK), 0)
    s_idx = lax.broadcasted_iota(jnp.int32, (CHUNK, CHUNK), 1)
    causal = s_idx <= t_idx
    for g in range(A_GROUPS):
        cols = slice(g * A_GROUP_WIDTH, (g + 1) * A_GROUP_WIDTH)
        u = _gelu_tanh(_dot(h, win_ref[:, cols]))
        z = _dot(h, win_ref[:, 2 * A_WIDTH + g * A_GROUP_WIDTH:2 * A_WIDTH + (g + 1) * A_GROUP_WIDTH])
        w_causal = jnp.where(causal, ws_ref[g], 0.0).astype(BF16)
        bias = bs_ref[:, g:g + 1]
        sg = sg_ref[:, cols]
        mixed = []
        for c in range(tile // CHUNK):
            rows = slice(c * CHUNK, (c + 1) * CHUNK)
            vn = (gv_ref[rows, cols] * inv[rows] * sg).astype(BF16)
            mixed.append(_dot(w_causal, vn) + bias)
        mixed = jnp.concatenate(mixed, axis=0)
        y_ref[:, cols] = (u * mixed * _silu(z)).astype(BF16)

    o_ref[0] = x + mod[2:3] * _dot(y_ref[...], wout_ref[...])


def _gmlp_layer(x, mod, norm_g, w_in, sgu_g, w_s, b_s_t, w_out):
    bsz, s, d = x.shape
    t = TOKEN_TILE
    return pl.pallas_call(
        _gmlp_kernel,
        grid=(bsz, s // t),
        in_specs=[
            pl.BlockSpec((1, t, d), lambda b, i: (b, i, 0)),
            pl.BlockSpec((1, 3, d), lambda b, i: (b, 0, 0)),
            _resident((1, d)),
            _resident((d, 3 * A_WIDTH)),
            _resident((1, A_WIDTH)),
            _resident((A_GROUPS, CHUNK, CHUNK)),
            _resident((CHUNK, A_GROUPS)),
            _resident((A_WIDTH, d)),
        ],
        out_specs=pl.BlockSpec((1, t, d), lambda b, i: (b, i, 0)),
        out_shape=jax.ShapeDtypeStruct(x.shape, x.dtype),
        scratch_shapes=[pltpu.VMEM((t, A_WIDTH), F32), pltpu.VMEM((t, A_WIDTH), BF16)],
        compiler_params=_params(),
        name="gmlp",
    )(x, mod, norm_g, w_in, sgu_g, w_s, b_s_t, w_out)


def _kv_kernel(x_ref, mod_ref, ng_ref, w_ref, kg_ref, k_ref, v_ref):
    mod = mod_ref[0]
    h = _mod_norm(x_ref[0], ng_ref[...], mod[1:2], mod[0:1]).astype(BF16)
    kv = _dot(h, w_ref[...])
    kg = kg_ref[...]
    for hd in range(N_KV_HEADS):
        cols = slice(hd * HEAD_DIM, (hd + 1) * HEAD_DIM)
        k_ref[0, :, cols] = _rms(kv[:, cols], kg).astype(BF16)
    v_ref[0] = kv[:, KV_WIDTH:].astype(BF16)


def _shared_kv(x, mod, norm_g, w_kv, k_norm_g):
    bsz, s, d = x.shape
    t = TOKEN_TILE
    out = jax.ShapeDtypeStruct((bsz, s, KV_WIDTH), BF16)
    return pl.pallas_call(
        _kv_kernel,
        grid=(bsz, s // t),
        in_specs=[
            pl.BlockSpec((1, t, d), lambda b, i: (b, i, 0)),
            pl.BlockSpec((1, 2, d), lambda b, i: (b, 0, 0)),
            _resident((1, d)),
            _resident((d, 2 * KV_WIDTH)),
            _resident((1, HEAD_DIM)),
        ],
        out_specs=[pl.BlockSpec((1, t, KV_WIDTH), lambda b, i: (b, i, 0))] * 2,
        out_shape=[out, out],
        compiler_params=_params(),
        name="shared_kv",
    )(x, mod, norm_g, w_kv, k_norm_g)


QZ_COL_CHUNK = 512


def _qz_kernel(x_ref, mod_ref, ng_ref, w_ref, qg_ref, q_ref, sz_ref):
    mod = mod_ref[0]
    h = _mod_norm(x_ref[0], ng_ref[...], mod[1:2], mod[0:1]).astype(BF16)
    qg = qg_ref[...] * (HEAD_DIM ** -0.5)
    for j in range(Q_WIDTH // QZ_COL_CHUNK):
        q = _dot(h, w_ref[:, j * QZ_COL_CHUNK:(j + 1) * QZ_COL_CHUNK])
        for hd in range(QZ_COL_CHUNK // HEAD_DIM):
            cols = slice(hd * HEAD_DIM, (hd + 1) * HEAD_DIM)
            q_ref[0, :, j * QZ_COL_CHUNK + hd * HEAD_DIM:j * QZ_COL_CHUNK + (hd + 1) * HEAD_DIM] = (
                _rms(q[:, cols], qg)).astype(BF16)
    for j in range(B_WIDTH // QZ_COL_CHUNK):
        cols = slice(Q_WIDTH + j * QZ_COL_CHUNK, Q_WIDTH + (j + 1) * QZ_COL_CHUNK)
        sz_ref[0, :, j * QZ_COL_CHUNK:(j + 1) * QZ_COL_CHUNK] = _silu(_dot(h, w_ref[:, cols])).astype(BF16)


def _qz_proj(x, mod, norm_g, w_in, q_norm_g):
    bsz, s, d = x.shape
    t = TOKEN_TILE
    return pl.pallas_call(
        _qz_kernel,
        grid=(bsz, s // t),
        in_specs=[
            pl.BlockSpec((1, t, d), lambda b, i: (b, i, 0)),
            pl.BlockSpec((1, 3, d), lambda b, i: (b, 0, 0)),
            _resident((1, d)),
            _resident((d, Q_WIDTH + B_WIDTH)),
            _resident((1, HEAD_DIM)),
        ],
        out_specs=[pl.BlockSpec((1, t, Q_WIDTH), lambda b, i: (b, i, 0)),
                   pl.BlockSpec((1, t, B_WIDTH), lambda b, i: (b, i, 0))],
        out_shape=[jax.ShapeDtypeStruct((bsz, s, Q_WIDTH), BF16),
                   jax.ShapeDtypeStruct((bsz, s, B_WIDTH), BF16)],
        compiler_params=_params(),
        name="qz_proj",
    )(x, mod, norm_g, w_in, q_norm_g)


def _alibi_slope(head):
    return 2.0 ** (-8.0 * (head + 1) / N_Q_HEADS)


def _attn_kernel(q_ref, kp_ref, kc_ref, vp_ref, vc_ref, o_ref, lse_ref, bias_ref, *, group):
    window, dil = DILATED_GROUPS[group]
    first = (pl.program_id(0) == 0) & (pl.program_id(1) == 0) & (pl.program_id(2) == 0)
    n = pl.program_id(2)

    @pl.when(first)
    def _():
        i = lax.broadcasted_iota(jnp.int32, (BAND, 2 * BAND), 0)
        j = lax.broadcasted_iota(jnp.int32, (BAND, 2 * BAND), 1)
        dq = BAND + i - j
        in_band = (dq >= 0) & (dq <= window // dil)
        dist = (dil * dq).astype(F32)
        for kv in range(KV_HEADS_PER_GROUP):
            for hq in range(Q_PER_KV):
                slope = _alibi_slope(group * Q_HEADS_PER_GROUP + kv * Q_PER_KV + hq)
                rows = slice(hq * BAND, (hq + 1) * BAND)
                bias_ref[1, kv, rows, :] = jnp.where(in_band, -slope * dist, MASKED)
                bias_ref[0, kv, rows, :] = jnp.where(in_band & (j >= BAND), -slope * dist, MASKED)

    has_prev = jnp.minimum(n, 1)
    lane = lax.broadcasted_iota(jnp.int32, (BAND, LANES), 1)
    lse_tile = jnp.zeros((BAND, LANES), F32)
    for kv in range(KV_HEADS_PER_GROUP):
        kcols = slice(kv * HEAD_DIM, (kv + 1) * HEAD_DIM)
        k = jnp.concatenate([kp_ref[0, :, kcols], kc_ref[0, :, kcols]], axis=0)
        v = jnp.concatenate([vp_ref[0, :, kcols], vc_ref[0, :, kcols]], axis=0)
        q = jnp.concatenate(
            [q_ref[0, :, (kv * Q_PER_KV + hq) * HEAD_DIM:(kv * Q_PER_KV + hq + 1) * HEAD_DIM]
             for hq in range(Q_PER_KV)], axis=0)
        s = lax.dot_general(q, k, (((1,), (1,)), ((), ())), preferred_element_type=F32)
        s = s + bias_ref[has_prev, kv]
        m = jnp.max(s, axis=-1, keepdims=True)
        p = jnp.exp(s - m)
        l = jnp.sum(p, axis=-1, keepdims=True)
        o = _dot(p.astype(BF16), v) / l
        lse = m + jnp.log(l)
        for hq in range(Q_PER_KV):
            head = kv * Q_PER_KV + hq
            rows = slice(hq * BAND, (hq + 1) * BAND)
            o_ref[0, :, head * HEAD_DIM:(head + 1) * HEAD_DIM] = o[rows].astype(BF16)
            lse_tile = jnp.where(lane == head, lse[rows], lse_tile)
    lse_ref[0] = lse_tile


def _dilated_attn(q, k, v, group):
    _, dil = DILATED_GROUPS[group]
    bsz, s, _ = q.shape
    sd = s // dil
    nb = sd // BAND
    qv = q.reshape(bsz, sd, dil * Q_WIDTH)
    kv_ = k.reshape(bsz, sd, dil * KV_WIDTH)
    vv = v.reshape(bsz, sd, dil * KV_WIDTH)
    n_qblk = Q_WIDTH // B_WIDTH
    n_kblk = KV_WIDTH // KV_GROUP_WIDTH
    cur = lambda b, r, n: (b, n, r * n_kblk + group)
    prev = lambda b, r, n: (b, jnp.maximum(n - 1, 0), r * n_kblk + group)
    kspec = lambda im: pl.BlockSpec((1, BAND, KV_GROUP_WIDTH), im)
    o, lse = pl.pallas_call(
        functools.partial(_attn_kernel, group=group),
        grid=(bsz, dil, nb),
        in_specs=[
            pl.BlockSpec((1, BAND, B_WIDTH), lambda b, r, n: (b, n, r * n_qblk + group)),
            kspec(prev), kspec(cur), kspec(prev), kspec(cur),
        ],
        out_specs=[pl.BlockSpec((1, BAND, B_WIDTH), lambda b, r, n: (b, n, r)),
                   pl.BlockSpec((1, BAND, LANES), lambda b, r, n: (b, n, r))],
        out_shape=[jax.ShapeDtypeStruct((bsz, sd, dil * B_WIDTH), BF16),
                   jax.ShapeDtypeStruct((bsz, sd, dil * LANES), F32)],
        scratch_shapes=[pltpu.VMEM((2, KV_HEADS_PER_GROUP, Q_PER_KV * BAND, 2 * BAND), F32)],
        compiler_params=pltpu.CompilerParams(
            dimension_semantics=("arbitrary", "arbitrary", "arbitrary"),
            vmem_limit_bytes=VMEM_LIMIT_BYTES),
        name=f"dilated_attn_g{group}",
    )(qv, kv_, kv_, vv, vv)
    return o.reshape(bsz, s, B_WIDTH), lse.reshape(bsz, s, LANES)


def _merge_kernel(x_ref, mod_ref, sz_ref, o0_ref, o1_ref, o2_ref, l0_ref, l1_ref, l2_ref, w_ref,
                  out_ref, y_ref):
    lses = [l0_ref[0], l1_ref[0], l2_ref[0]]
    top = jnp.maximum(jnp.maximum(lses[0], lses[1]), lses[2])
    es = [jnp.exp(l - top) for l in lses]
    den = es[0] + es[1] + es[2]
    alphas = [e / den for e in es]
    o_refs = (o0_ref, o1_ref, o2_ref)
    for hd in range(Q_HEADS_PER_GROUP):
        cols = slice(hd * HEAD_DIM, (hd + 1) * HEAD_DIM)
        o = alphas[0][:, hd:hd + 1] * o_refs[0][0, :, cols].astype(F32)
        for g in range(1, N_GROUPS):
            o = o + alphas[g][:, hd:hd + 1] * o_refs[g][0, :, cols].astype(F32)
        y_ref[:, cols] = (o * sz_ref[0, :, cols].astype(F32)).astype(BF16)
    out_ref[0] = x_ref[0] + mod_ref[0][2:3] * _dot(y_ref[...], w_ref[...])


def _merge_out(x, mod, sz, os_, lses, w_out):
    bsz, s, d = x.shape
    t = TOKEN_TILE
    tok = lambda w: pl.BlockSpec((1, t, w), lambda b, i: (b, i, 0))
    return pl.pallas_call(
        _merge_kernel,
        grid=(bsz, s // t),
        in_specs=[tok(d), pl.BlockSpec((1, 3, d), lambda b, i: (b, 0, 0)), tok(B_WIDTH)]
        + [tok(B_WIDTH)] * N_GROUPS + [tok(LANES)] * N_GROUPS + [_resident((B_WIDTH, d))],
        out_specs=tok(d),
        out_shape=jax.ShapeDtypeStruct(x.shape, x.dtype),
        scratch_shapes=[pltpu.VMEM((t, B_WIDTH), BF16)],
        compiler_params=_params(),
        name="merge_out",
    )(x, mod, sz, *os_, *lses, w_out)


def kernel(x, c, a_ada_w, a_ada_b, a_norm_g, a_w_in, a_sgu_g, a_w_spatial, a_b_spatial, a_w_out,
           kv_ada_w, kv_ada_b, kv_norm_g, w_kv, k_norm_g, b_ada_w, b_ada_b, b_norm_g, b_w_in,
           b_q_norm_g, b_w_out):
    bsz = x.shape[0]
    d = D_MODEL
    assert x.shape[1] % (TOKEN_TILE) == 0 and all(x.shape[1] % (dil * BAND) == 0 for _, dil in DILATED_GROUPS)

    a_mod = _ada(c, a_ada_w, a_ada_b).reshape(N_A_LAYERS, bsz, 3, d)
    kv_mod = _ada(c, kv_ada_w[None], kv_ada_b[None]).reshape(bsz, 2, d)
    b_mod = _ada(c, b_ada_w, b_ada_b).reshape(N_B_LAYERS, bsz, 3, d)

    a_w_in16 = a_w_in.astype(BF16)
    a_w_out16 = a_w_out.astype(BF16)
    b_s_t = jnp.swapaxes(a_b_spatial, 1, 2)
    for layer in range(N_A_LAYERS):
        x = _gmlp_layer(x, a_mod[layer], a_norm_g[layer][None], a_w_in16[layer], a_sgu_g[layer][None],
                        a_w_spatial[layer], b_s_t[layer], a_w_out16[layer])

    k, v = _shared_kv(x, kv_mod, kv_norm_g[None], w_kv.astype(BF16), k_norm_g[None])

    b_w_in16 = b_w_in.astype(BF16)
    b_w_out16 = b_w_out.astype(BF16)
    for layer in range(N_B_LAYERS):
        q, sz = _qz_proj(x, b_mod[layer], b_norm_g[layer][None], b_w_in16[layer], b_q_norm_g[layer][None])
        outs, lses = zip(*[_dilated_attn(q, k, v, g) for g in range(N_GROUPS)])
        x = _merge_out(x, b_mod[layer], sz, outs, lses, b_w_out16[layer])
    return x
```

```python
import functools
import math

import jax
import jax.numpy as jnp
from jax import lax
from jax.experimental import pallas as pl
from jax.experimental.pallas import tpu as pltpu

D_MODEL = 1024
DEPTH = 4
N_A_LAYERS = DEPTH // 2
N_B_LAYERS = DEPTH - N_A_LAYERS
EPS = 1e-6

CHUNK = 128
A_WIDTH = 2 * D_MODEL
A_GROUP_WIDTH = 256
A_GROUPS = A_WIDTH // A_GROUP_WIDTH

HEAD_DIM = 128
DILATED_GROUPS = ((128, 1), (512, 4), (2048, 16))
N_GROUPS = len(DILATED_GROUPS)
Q_HEADS_PER_GROUP = D_MODEL // HEAD_DIM
KV_HEADS_PER_GROUP = 2
Q_PER_KV = Q_HEADS_PER_GROUP // KV_HEADS_PER_GROUP
N_Q_HEADS = N_GROUPS * Q_HEADS_PER_GROUP
N_KV_HEADS = N_GROUPS * KV_HEADS_PER_GROUP
B_WIDTH = Q_HEADS_PER_GROUP * HEAD_DIM
BAND = 128
Q_WIDTH = N_Q_HEADS * HEAD_DIM
KV_WIDTH = N_KV_HEADS * HEAD_DIM
KV_GROUP_WIDTH = KV_HEADS_PER_GROUP * HEAD_DIM

V7X_VMEM_BYTES = 64 * 1024 * 1024
VMEM_LIMIT_BYTES = V7X_VMEM_BYTES - 8 * 1024 * 1024
LANES = 128

TOKEN_TILE = 512
MASKED = -1e30

F32 = jnp.float32
BF16 = jnp.bfloat16


def _params():
    return pltpu.CompilerParams(vmem_limit_bytes=VMEM_LIMIT_BYTES)


def _resident(shape):
    return pl.BlockSpec(shape, lambda *_: (0,) * len(shape), pipeline_mode=pl.Buffered(1))


def _gelu_tanh(x):
    return 0.5 * x * (1.0 + jnp.tanh(math.sqrt(2.0 / math.pi) * (x + 0.044715 * (x * x * x))))


def _silu(x):
    return x / (1.0 + jnp.exp(-x))


def _rms(x, g):
    return x * lax.rsqrt(jnp.mean(x * x, axis=-1, keepdims=True) + EPS) * g


def _mod_norm(x, g, scale, shift):
    return _rms(x, g) * (1.0 + scale) + shift


def _dot(a, b):
    return jnp.dot(a, b, preferred_element_type=F32)


def _ada_kernel(c_ref, w_ref, b_ref, o_ref):
    a = _silu(c_ref[...]).astype(BF16)
    o_ref[0] = _dot(a, w_ref[0].astype(BF16)) + b_ref[0]


def _ada(c, w, b):
    n_layers, d, n = w.shape
    bsz = c.shape[0]
    bn = 1024
    return pl.pallas_call(
        _ada_kernel,
        grid=(n_layers, n // bn),
        in_specs=[
            pl.BlockSpec((bsz, d), lambda l, j: (0, 0)),
            pl.BlockSpec((1, d, bn), lambda l, j: (l, 0, j)),
            pl.BlockSpec((1, 1, bn), lambda l, j: (l, 0, j)),
        ],
        out_specs=pl.BlockSpec((1, bsz, bn), lambda l, j: (l, 0, j)),
        out_shape=jax.ShapeDtypeStruct((n_layers, bsz, n), F32),
        compiler_params=_params(),
        name="ada",
    )(c, w, b.reshape(n_layers, 1, n))


def _gmlp_kernel(x_ref, mod_ref, ng_ref, win_ref, sg_ref, ws_ref, bs_ref, wout_ref, o_ref,
                 gv_ref, y_ref):
    x = x_ref[0]
    mod = mod_ref[0]
    h = _mod_norm(x, ng_ref[...], mod[1:2], mod[0:1]).astype(BF16)
    tile = x.shape[0]

    ssq = jnp.zeros((tile, 1), F32)
    for g in range(A_GROUPS):
        cols = slice(g * A_GROUP_WIDTH, (g + 1) * A_GROUP_WIDTH)
        gv = _gelu_tanh(_dot(h, win_ref[:, A_WIDTH + g * A_GROUP_WIDTH:A_WIDTH + (g + 1) * A_GROUP_WIDTH]))
        ssq = ssq + jnp.sum(gv * gv, axis=-1, keepdims=True)
        gv_ref[:, cols] = gv
    inv = lax.rsqrt(ssq * (1.0 / A_WIDTH) + EPS)

    t_idx = lax.broadcasted_iota(jnp.int32, (CHUNK, CHUNK), 0)
    s_idx = lax.broadcasted_iota(jnp.int32, (CHUNK, CHUNK), 1)
    causal = s_idx <= t_idx
    for g in range(A_GROUPS):
        cols = slice(g * A_GROUP_WIDTH, (g + 1) * A_GROUP_WIDTH)
        u = _gelu_tanh(_dot(h, win_ref[:, cols]))
        z = _dot(h, win_ref[:, 2 * A_WIDTH + g * A_GROUP_WIDTH:2 * A_WIDTH + (g + 1) * A_GROUP_WIDTH])
        w_causal = jnp.where(causal, ws_ref[g], 0.0).astype(BF16)
        bias = bs_ref[:, g:g + 1]
        sg = sg_ref[:, cols]
        mixed = []
        for c in range(tile // CHUNK):
            rows = slice(c * CHUNK, (c + 1) * CHUNK)
            vn = (gv_ref[rows, cols] * inv[rows] * sg).astype(BF16)
            mixed.append(_dot(w_causal, vn) + bias)
        mixed = jnp.concatenate(mixed, axis=0)
        y_ref[:, cols] = (u * mixed * _silu(z)).astype(BF16)

    o_ref[0] = x + mod[2:3] * _dot(y_ref[...], wout_ref[...])


def _gmlp_layer(x, mod, norm_g, w_in, sgu_g, w_s, b_s_t, w_out):
    bsz, s, d = x.shape
    t = TOKEN_TILE
    return pl.pallas_call(
        _gmlp_kernel,
        grid=(bsz, s // t),
        in_specs=[
            pl.BlockSpec((1, t, d), lambda b, i: (b, i, 0)),
            pl.BlockSpec((1, 3, d), lambda b, i: (b, 0, 0)),
            _resident((1, d)),
            _resident((d, 3 * A_WIDTH)),
            _resident((1, A_WIDTH)),
            _resident((A_GROUPS, CHUNK, CHUNK)),
            _resident((CHUNK, A_GROUPS)),
            _resident((A_WIDTH, d)),
        ],
        out_specs=pl.BlockSpec((1, t, d), lambda b, i: (b, i, 0)),
        out_shape=jax.ShapeDtypeStruct(x.shape, x.dtype),
        scratch_shapes=[pltpu.VMEM((t, A_WIDTH), F32), pltpu.VMEM((t, A_WIDTH), BF16)],
        compiler_params=_params(),
        name="gmlp",
    )(x, mod, norm_g, w_in, sgu_g, w_s, b_s_t, w_out)


def _dilations():
    return tuple(dil for _, dil in DILATED_GROUPS)


def _dilated_shape(bsz, s, dil, width):
    return (bsz, dil, s // dil, width)


def _dilated_block(tile, dil, width):
    return pl.BlockSpec((1, dil, tile // dil, width), lambda b, i: (b, 0, i, 0))


def _split_residues(stage_ref, heads, out_ref, dil):
    rows = stage_ref.shape[1] // dil
    for r in range(dil):
        for i, hd in enumerate(heads):
            src = stage_ref[hd] if dil == 1 else stage_ref[hd, pl.ds(r, rows, stride=dil), :]
            out_ref[0, r, :, i * LANES:(i + 1) * LANES] = src.astype(out_ref.dtype)


def _kv_kernel(x_ref, mod_ref, ng_ref, w_ref, kg_ref, k0_ref, k1_ref, k2_ref, v0_ref, v1_ref, v2_ref,
               kstage_ref, vstage_ref):
    mod = mod_ref[0]
    h = _mod_norm(x_ref[0], ng_ref[...], mod[1:2], mod[0:1]).astype(BF16)
    kv = _dot(h, w_ref[...])
    kg = kg_ref[...]
    for hd in range(N_KV_HEADS):
        cols = slice(hd * HEAD_DIM, (hd + 1) * HEAD_DIM)
        kstage_ref[hd] = _rms(kv[:, cols], kg)
        vstage_ref[hd] = kv[:, KV_WIDTH + hd * HEAD_DIM:KV_WIDTH + (hd + 1) * HEAD_DIM]
    for g, (k_ref, v_ref) in enumerate(((k0_ref, v0_ref), (k1_ref, v1_ref), (k2_ref, v2_ref))):
        heads = range(g * KV_HEADS_PER_GROUP, (g + 1) * KV_HEADS_PER_GROUP)
        _split_residues(kstage_ref, heads, k_ref, _dilations()[g])
        _split_residues(vstage_ref, heads, v_ref, _dilations()[g])


def _shared_kv(x, mod, norm_g, w_kv, k_norm_g):
    bsz, s, d = x.shape
    t = TOKEN_TILE
    outs = [jax.ShapeDtypeStruct(_dilated_shape(bsz, s, dil, KV_GROUP_WIDTH), BF16) for dil in _dilations()]
    specs = [_dilated_block(t, dil, KV_GROUP_WIDTH) for dil in _dilations()]
    res = pl.pallas_call(
        _kv_kernel,
        grid=(bsz, s // t),
        in_specs=[
            pl.BlockSpec((1, t, d), lambda b, i: (b, i, 0)),
            pl.BlockSpec((1, 2, d), lambda b, i: (b, 0, 0)),
            _resident((1, d)),
            _resident((d, 2 * KV_WIDTH)),
            _resident((1, HEAD_DIM)),
        ],
        out_specs=specs + specs,
        out_shape=outs + outs,
        scratch_shapes=[pltpu.VMEM((N_KV_HEADS, t, HEAD_DIM), F32)] * 2,
        compiler_params=_params(),
        name="shared_kv",
    )(x, mod, norm_g, w_kv, k_norm_g)
    return res[:N_GROUPS], res[N_GROUPS:]


QZ_COL_CHUNK = 512


def _qz_kernel(x_ref, mod_ref, ng_ref, w_ref, qg_ref, q0_ref, q1_ref, q2_ref, sz_ref, stage_ref):
    mod = mod_ref[0]
    h = _mod_norm(x_ref[0], ng_ref[...], mod[1:2], mod[0:1]).astype(BF16)
    qg = qg_ref[...] * (HEAD_DIM ** -0.5)
    q_refs = (q0_ref, q1_ref, q2_ref)
    heads_per_chunk = QZ_COL_CHUNK // HEAD_DIM
    for j in range(Q_WIDTH // QZ_COL_CHUNK):
        q = _dot(h, w_ref[:, j * QZ_COL_CHUNK:(j + 1) * QZ_COL_CHUNK])
        for hd in range(heads_per_chunk):
            cols = slice(hd * HEAD_DIM, (hd + 1) * HEAD_DIM)
            stage_ref[j * heads_per_chunk + hd] = _rms(q[:, cols], qg)
    for g in range(N_GROUPS):
        heads = range(g * Q_HEADS_PER_GROUP, (g + 1) * Q_HEADS_PER_GROUP)
        _split_residues(stage_ref, heads, q_refs[g], _dilations()[g])
    for j in range(B_WIDTH // QZ_COL_CHUNK):
        cols = slice(Q_WIDTH + j * QZ_COL_CHUNK, Q_WIDTH + (j + 1) * QZ_COL_CHUNK)
        sz_ref[0, :, j * QZ_COL_CHUNK:(j + 1) * QZ_COL_CHUNK] = _silu(_dot(h, w_ref[:, cols])).astype(BF16)


def _qz_proj(x, mod, norm_g, w_in, q_norm_g):
    bsz, s, d = x.shape
    t = TOKEN_TILE
    return pl.pallas_call(
        _qz_kernel,
        grid=(bsz, s // t),
        in_specs=[
            pl.BlockSpec((1, t, d), lambda b, i: (b, i, 0)),
            pl.BlockSpec((1, 3, d), lambda b, i: (b, 0, 0)),
            _resident((1, d)),
            _resident((d, Q_WIDTH + B_WIDTH)),
            _resident((1, HEAD_DIM)),
        ],
        out_specs=[_dilated_block(t, dil, B_WIDTH) for dil in _dilations()]
        + [pl.BlockSpec((1, t, B_WIDTH), lambda b, i: (b, i, 0))],
        out_shape=[jax.ShapeDtypeStruct(_dilated_shape(bsz, s, dil, B_WIDTH), BF16) for dil in _dilations()]
        + [jax.ShapeDtypeStruct((bsz, s, B_WIDTH), BF16)],
        scratch_shapes=[pltpu.VMEM((N_Q_HEADS, t, HEAD_DIM), F32)],
        compiler_params=_params(),
        name="qz_proj",
    )(x, mod, norm_g, w_in, q_norm_g)


def _alibi_slope(head):
    return 2.0 ** (-8.0 * (head + 1) / N_Q_HEADS)


def _attn_kernel(q_ref, kp_ref, kc_ref, vp_ref, vc_ref, o_ref, lse_ref, bias_ref, *, group):
    window, dil = DILATED_GROUPS[group]
    first = (pl.program_id(0) == 0) & (pl.program_id(1) == 0) & (pl.program_id(2) == 0)
    n = pl.program_id(2)

    @pl.when(first)
    def _():
        i = lax.broadcasted_iota(jnp.int32, (BAND, 2 * BAND), 0)
        j = lax.broadcasted_iota(jnp.int32, (BAND, 2 * BAND), 1)
        dq = BAND + i - j
        in_band = (dq >= 0) & (dq <= window // dil)
        dist = (dil * dq).astype(F32)
        for kv in range(KV_HEADS_PER_GROUP):
            for hq in range(Q_PER_KV):
                slope = _alibi_slope(group * Q_HEADS_PER_GROUP + kv * Q_PER_KV + hq)
                rows = slice(hq * BAND, (hq + 1) * BAND)
                bias_ref[1, kv, rows, :] = jnp.where(in_band, -slope * dist, MASKED)
                bias_ref[0, kv, rows, :] = jnp.where(in_band & (j >= BAND), -slope * dist, MASKED)

    has_prev = jnp.minimum(n, 1)
    lane = lax.broadcasted_iota(jnp.int32, (BAND, LANES), 1)
    lse_tile = jnp.zeros((BAND, LANES), F32)
    for kv in range(KV_HEADS_PER_GROUP):
        kcols = slice(kv * HEAD_DIM, (kv + 1) * HEAD_DIM)
        k = jnp.concatenate([kp_ref[0, 0, :, kcols], kc_ref[0, 0, :, kcols]], axis=0)
        v = jnp.concatenate([vp_ref[0, 0, :, kcols], vc_ref[0, 0, :, kcols]], axis=0)
        q = jnp.concatenate(
            [q_ref[0, 0, :, (kv * Q_PER_KV + hq) * HEAD_DIM:(kv * Q_PER_KV + hq + 1) * HEAD_DIM]
             for hq in range(Q_PER_KV)], axis=0)
        s = lax.dot_general(q, k, (((1,), (1,)), ((), ())), preferred_element_type=F32)
        s = s + bias_ref[has_prev, kv]
        m = jnp.max(s, axis=-1, keepdims=True)
        p = jnp.exp(s - m)
        l = jnp.sum(p, axis=-1, keepdims=True)
        o = _dot(p.astype(BF16), v) / l
        lse = m + jnp.log(l)
        for hq in range(Q_PER_KV):
            head = kv * Q_PER_KV + hq
            rows = slice(hq * BAND, (hq + 1) * BAND)
            o_ref[0, 0, :, head * HEAD_DIM:(head + 1) * HEAD_DIM] = o[rows].astype(BF16)
            lse_tile = jnp.where(lane == head, lse[rows], lse_tile)
    lse_ref[0, 0] = lse_tile


def _dilated_attn(q, k, v, group):
    bsz, dil, sd, _ = q.shape
    nb = sd // BAND
    cur = lambda b, r, n: (b, r, n, 0)
    prev = lambda b, r, n: (b, r, jnp.maximum(n - 1, 0), 0)
    kspec = lambda im: pl.BlockSpec((1, 1, BAND, KV_GROUP_WIDTH), im)
    return pl.pallas_call(
        functools.partial(_attn_kernel, group=group),
        grid=(bsz, dil, nb),
        in_specs=[pl.BlockSpec((1, 1, BAND, B_WIDTH), cur), kspec(prev), kspec(cur), kspec(prev), kspec(cur)],
        out_specs=[pl.BlockSpec((1, 1, BAND, B_WIDTH), cur), pl.BlockSpec((1, 1, BAND, LANES), cur)],
        out_shape=[jax.ShapeDtypeStruct(q.shape, BF16), jax.ShapeDtypeStruct((bsz, dil, sd, LANES), F32)],
        scratch_shapes=[pltpu.VMEM((2, KV_HEADS_PER_GROUP, Q_PER_KV * BAND, 2 * BAND), F32)],
        compiler_params=pltpu.CompilerParams(
            dimension_semantics=("arbitrary", "arbitrary", "arbitrary"),
            vmem_limit_bytes=VMEM_LIMIT_BYTES),
        name=f"dilated_attn_g{group}",
    )(q, k, k, v, v)


def _merge_kernel(x_ref, mod_ref, sz_ref, o0_ref, o1_ref, o2_ref, l0_ref, l1_ref, l2_ref, w_ref,
                  out_ref, y_ref, ostage_ref, lstage_ref):
    for g, (o_ref, l_ref) in enumerate(((o0_ref, l0_ref), (o1_ref, l1_ref), (o2_ref, l2_ref))):
        dil = _dilations()[g]
        rows = o_ref.shape[2]
        for r in range(dil):
            dst = slice(None) if dil == 1 else pl.ds(r, rows, stride=dil)
            for hd in range(Q_HEADS_PER_GROUP):
                ostage_ref[g, hd, dst, :] = o_ref[0, r, :, hd * HEAD_DIM:(hd + 1) * HEAD_DIM].astype(F32)
            lstage_ref[g, dst, :] = l_ref[0, r]
    lses = [lstage_ref[g] for g in range(N_GROUPS)]
    top = jnp.maximum(jnp.maximum(lses[0], lses[1]), lses[2])
    es = [jnp.exp(l - top) for l in lses]
    den = es[0] + es[1] + es[2]
    alphas = [e / den for e in es]
    for hd in range(Q_HEADS_PER_GROUP):
        cols = slice(hd * HEAD_DIM, (hd + 1) * HEAD_DIM)
        o = alphas[0][:, hd:hd + 1] * ostage_ref[0, hd]
        for g in range(1, N_GROUPS):
            o = o + alphas[g][:, hd:hd + 1] * ostage_ref[g, hd]
        y_ref[:, cols] = (o * sz_ref[0, :, cols].astype(F32)).astype(BF16)
    out_ref[0] = x_ref[0] + mod_ref[0][2:3] * _dot(y_ref[...], w_ref[...])


def _merge_out(x, mod, sz, os_, lses, w_out):
    bsz, s, d = x.shape
    t = TOKEN_TILE
    tok = lambda w: pl.BlockSpec((1, t, w), lambda b, i: (b, i, 0))
    return pl.pallas_call(
        _merge_kernel,
        grid=(bsz, s // t),
        in_specs=[tok(d), pl.BlockSpec((1, 3, d), lambda b, i: (b, 0, 0)), tok(B_WIDTH)]
        + [_dilated_block(t, dil, B_WIDTH) for dil in _dilations()]
        + [_dilated_block(t, dil, LANES) for dil in _dilations()]
        + [_resident((B_WIDTH, d))],
        out_specs=tok(d),
        out_shape=jax.ShapeDtypeStruct(x.shape, x.dtype),
        scratch_shapes=[pltpu.VMEM((t, B_WIDTH), BF16),
                        pltpu.VMEM((N_GROUPS, Q_HEADS_PER_GROUP, t, HEAD_DIM), F32),
                        pltpu.VMEM((N_GROUPS, t, LANES), F32)],
        compiler_params=_params(),
        name="merge_out",
    )(x, mod, sz, *os_, *lses, w_out)


def kernel(x, c, a_ada_w, a_ada_b, a_norm_g, a_w_in, a_sgu_g, a_w_spatial, a_b_spatial, a_w_out,
           kv_ada_w, kv_ada_b, kv_norm_g, w_kv, k_norm_g, b_ada_w, b_ada_b, b_norm_g, b_w_in,
           b_q_norm_g, b_w_out):
    bsz = x.shape[0]
    d = D_MODEL
    assert x.shape[1] % TOKEN_TILE == 0 and all(TOKEN_TILE % dil == 0 for dil in _dilations())
    assert all(x.shape[1] % (dil * BAND) == 0 for dil in _dilations())

    a_mod = _ada(c, a_ada_w, a_ada_b).reshape(N_A_LAYERS, bsz, 3, d)
    kv_mod = _ada(c, kv_ada_w[None], kv_ada_b[None]).reshape(bsz, 2, d)
    b_mod = _ada(c, b_ada_w, b_ada_b).reshape(N_B_LAYERS, bsz, 3, d)

    a_w_in16 = a_w_in.astype(BF16)
    a_w_out16 = a_w_out.astype(BF16)
    b_s_t = jnp.swapaxes(a_b_spatial, 1, 2)
    for layer in range(N_A_LAYERS):
        x = _gmlp_layer(x, a_mod[layer], a_norm_g[layer][None], a_w_in16[layer], a_sgu_g[layer][None],
                        a_w_spatial[layer], b_s_t[layer], a_w_out16[layer])

    ks, vs = _shared_kv(x, kv_mod, kv_norm_g[None], w_kv.astype(BF16), k_norm_g[None])

    b_w_in16 = b_w_in.astype(BF16)
    b_w_out16 = b_w_out.astype(BF16)
    for layer in range(N_B_LAYERS):
        *qs, sz = _qz_proj(x, b_mod[layer], b_norm_g[layer][None], b_w_in16[layer], b_q_norm_g[layer][None])
        outs, lses = zip(*[_dilated_attn(qs[g], ks[g], vs[g], g) for g in range(N_GROUPS)])
        x = _merge_out(x, b_mod[layer], sz, outs, lses, b_w_out16[layer])
    return x
```

```python
import functools
import math

import jax
import jax.numpy as jnp
from jax import lax
from jax.experimental import pallas as pl
from jax.experimental.pallas import tpu as pltpu

D_MODEL = 1024
DEPTH = 4
N_A_LAYERS = DEPTH // 2
N_B_LAYERS = DEPTH - N_A_LAYERS
EPS = 1e-6

CHUNK = 128
A_WIDTH = 2 * D_MODEL
A_GROUP_WIDTH = 256
A_GROUPS = A_WIDTH // A_GROUP_WIDTH

HEAD_DIM = 128
DILATED_GROUPS = ((128, 1), (512, 4), (2048, 16))
N_GROUPS = len(DILATED_GROUPS)
Q_HEADS_PER_GROUP = D_MODEL // HEAD_DIM
KV_HEADS_PER_GROUP = 2
Q_PER_KV = Q_HEADS_PER_GROUP // KV_HEADS_PER_GROUP
N_Q_HEADS = N_GROUPS * Q_HEADS_PER_GROUP
N_KV_HEADS = N_GROUPS * KV_HEADS_PER_GROUP
B_WIDTH = Q_HEADS_PER_GROUP * HEAD_DIM
BAND = 128
Q_WIDTH = N_Q_HEADS * HEAD_DIM
KV_WIDTH = N_KV_HEADS * HEAD_DIM
KV_GROUP_WIDTH = KV_HEADS_PER_GROUP * HEAD_DIM

V7X_VMEM_BYTES = 64 * 1024 * 1024
VMEM_LIMIT_BYTES = V7X_VMEM_BYTES - 8 * 1024 * 1024
LANES = 128

TOKEN_TILE = 512
MASKED = -1e30
LOG2_E = math.log2(math.e)
LN_2 = math.log(2.0)

F32 = jnp.float32
BF16 = jnp.bfloat16


def _params():
    return pltpu.CompilerParams(vmem_limit_bytes=VMEM_LIMIT_BYTES)


def _resident(shape):
    return pl.BlockSpec(shape, lambda *_: (0,) * len(shape), pipeline_mode=pl.Buffered(1))


def _gelu_tanh(x):
    return 0.5 * x * (1.0 + jnp.tanh(math.sqrt(2.0 / math.pi) * (x + 0.044715 * (x * x * x))))


def _silu(x):
    return x / (1.0 + jnp.exp(-x))


def _rms(x, g):
    return x * lax.rsqrt(jnp.mean(x * x, axis=-1, keepdims=True) + EPS) * g


def _mod_norm(x, g, scale, shift):
    return _rms(x, g) * (1.0 + scale) + shift


def _dot(a, b):
    return jnp.dot(a, b, preferred_element_type=F32)


def _ada_kernel(c_ref, w_ref, b_ref, o_ref):
    a = _silu(c_ref[...]).astype(BF16)
    o_ref[0] = _dot(a, w_ref[0].astype(BF16)) + b_ref[0]


def _ada(c, w, b):
    n_layers, d, n = w.shape
    bsz = c.shape[0]
    bn = 1024
    return pl.pallas_call(
        _ada_kernel,
        grid=(n_layers, n // bn),
        in_specs=[
            pl.BlockSpec((bsz, d), lambda l, j: (0, 0)),
            pl.BlockSpec((1, d, bn), lambda l, j: (l, 0, j)),
            pl.BlockSpec((1, 1, bn), lambda l, j: (l, 0, j)),
        ],
        out_specs=pl.BlockSpec((1, bsz, bn), lambda l, j: (l, 0, j)),
        out_shape=jax.ShapeDtypeStruct((n_layers, bsz, n), F32),
        compiler_params=_params(),
        name="ada",
    )(c, w, b.reshape(n_layers, 1, n))


def _gmlp_kernel(x_ref, mod_ref, ng_ref, win_ref, sg_ref, ws_ref, bs_ref, wout_ref, o_ref,
                 gv_ref, y_ref):
    x = x_ref[0]
    mod = mod_ref[0]
    h = _mod_norm(x, ng_ref[...], mod[1:2], mod[0:1]).astype(BF16)
    tile = x.shape[0]

    ssq = jnp.zeros((tile, 1), F32)
    for g in range(A_GROUPS):
        cols = slice(g * A_GROUP_WIDTH, (g + 1) * A_GROUP_WIDTH)
        gv = _gelu_tanh(_dot(h, win_ref[:, A_WIDTH + g * A_GROUP_WIDTH:A_WIDTH + (g + 1) * A_GROUP_WIDTH]))
        ssq = ssq + jnp.sum(gv * gv, axis=-1, keepdims=True)
        gv_ref[:, cols] = gv
    inv = lax.rsqrt(ssq * (1.0 / A_WIDTH) + EPS)

    t_idx = lax.broadcasted_iota(jnp.int32, (CHUNK, CHUNK), 0)
    s_idx = lax.broadcasted_iota(jnp.int32, (CHUNK, CHUNK), 1)
    causal = s_idx <= t_idx
    for g in range(A_GROUPS):
        cols = slice(g * A_GROUP_WIDTH, (g + 1) * A_GROUP_WIDTH)
        u = _gelu_tanh(_dot(h, win_ref[:, cols]))
        z = _dot(h, win_ref[:, 2 * A_WIDTH + g * A_GROUP_WIDTH:2 * A_WIDTH + (g + 1) * A_GROUP_WIDTH])
        w_causal = jnp.where(causal, ws_ref[g], 0.0).astype(BF16)
        bias = bs_ref[:, g:g + 1]
        sg = sg_ref[:, cols]
        mixed = []
        for c in range(tile // CHUNK):
            rows = slice(c * CHUNK, (c + 1) * CHUNK)
            vn = (gv_ref[rows, cols] * inv[rows] * sg).astype(BF16)
            mixed.append(_dot(w_causal, vn) + bias)
        mixed = jnp.concatenate(mixed, axis=0)
        y_ref[:, cols] = (u * mixed * _silu(z)).astype(BF16)

    o_ref[0] = x + mod[2:3] * _dot(y_ref[...], wout_ref[...])


def _gmlp_layer(x, mod, norm_g, w_in, sgu_g, w_s, b_s_t, w_out):
    bsz, s, d = x.shape
    t = TOKEN_TILE
    return pl.pallas_call(
        _gmlp_kernel,
        grid=(bsz, s // t),
        in_specs=[
            pl.BlockSpec((1, t, d), lambda b, i: (b, i, 0)),
            pl.BlockSpec((1, 3, d), lambda b, i: (b, 0, 0)),
            _resident((1, d)),
            _resident((d, 3 * A_WIDTH)),
            _resident((1, A_WIDTH)),
            _resident((A_GROUPS, CHUNK, CHUNK)),
            _resident((CHUNK, A_GROUPS)),
            _resident((A_WIDTH, d)),
        ],
        out_specs=pl.BlockSpec((1, t, d), lambda b, i: (b, i, 0)),
        out_shape=jax.ShapeDtypeStruct(x.shape, x.dtype),
        scratch_shapes=[pltpu.VMEM((t, A_WIDTH), F32), pltpu.VMEM((t, A_WIDTH), BF16)],
        compiler_params=_params(),
        name="gmlp",
    )(x, mod, norm_g, w_in, sgu_g, w_s, b_s_t, w_out)


def _dilations():
    return tuple(dil for _, dil in DILATED_GROUPS)


def _dilated_shape(bsz, s, dil, width):
    return (bsz, dil, s // dil, width)


def _dilated_block(tile, dil, width):
    return pl.BlockSpec((1, dil, tile // dil, width), lambda b, i: (b, 0, i, 0))


def _split_residues(stage_ref, heads, out_ref, dil):
    rows = stage_ref.shape[1] // dil
    for r in range(dil):
        for i, hd in enumerate(heads):
            src = stage_ref[hd] if dil == 1 else stage_ref[hd, pl.ds(r, rows, stride=dil), :]
            out_ref[0, r, :, i * LANES:(i + 1) * LANES] = src.astype(out_ref.dtype)


def _kv_kernel(x_ref, mod_ref, ng_ref, w_ref, kg_ref, k0_ref, k1_ref, k2_ref, v0_ref, v1_ref, v2_ref,
               kstage_ref, vstage_ref):
    mod = mod_ref[0]
    h = _mod_norm(x_ref[0], ng_ref[...], mod[1:2], mod[0:1]).astype(BF16)
    kv = _dot(h, w_ref[...])
    kg = kg_ref[...]
    for hd in range(N_KV_HEADS):
        cols = slice(hd * HEAD_DIM, (hd + 1) * HEAD_DIM)
        kstage_ref[hd] = _rms(kv[:, cols], kg)
        vstage_ref[hd] = kv[:, KV_WIDTH + hd * HEAD_DIM:KV_WIDTH + (hd + 1) * HEAD_DIM]
    for g, (k_ref, v_ref) in enumerate(((k0_ref, v0_ref), (k1_ref, v1_ref), (k2_ref, v2_ref))):
        heads = range(g * KV_HEADS_PER_GROUP, (g + 1) * KV_HEADS_PER_GROUP)
        _split_residues(kstage_ref, heads, k_ref, _dilations()[g])
        _split_residues(vstage_ref, heads, v_ref, _dilations()[g])


def _shared_kv(x, mod, norm_g, w_kv, k_norm_g):
    bsz, s, d = x.shape
    t = TOKEN_TILE
    outs = [jax.ShapeDtypeStruct(_dilated_shape(bsz, s, dil, KV_GROUP_WIDTH), BF16) for dil in _dilations()]
    specs = [_dilated_block(t, dil, KV_GROUP_WIDTH) for dil in _dilations()]
    res = pl.pallas_call(
        _kv_kernel,
        grid=(bsz, s // t),
        in_specs=[
            pl.BlockSpec((1, t, d), lambda b, i: (b, i, 0)),
            pl.BlockSpec((1, 2, d), lambda b, i: (b, 0, 0)),
            _resident((1, d)),
            _resident((d, 2 * KV_WIDTH)),
            _resident((1, HEAD_DIM)),
        ],
        out_specs=specs + specs,
        out_shape=outs + outs,
        scratch_shapes=[pltpu.VMEM((N_KV_HEADS, t, HEAD_DIM), F32)] * 2,
        compiler_params=_params(),
        name="shared_kv",
    )(x, mod, norm_g, w_kv, k_norm_g)
    return res[:N_GROUPS], res[N_GROUPS:]


QZ_COL_CHUNK = 512


def _qz_kernel(x_ref, mod_ref, ng_ref, w_ref, qg_ref, q0_ref, q1_ref, q2_ref, sz_ref, stage_ref):
    mod = mod_ref[0]
    h = _mod_norm(x_ref[0], ng_ref[...], mod[1:2], mod[0:1]).astype(BF16)
    qg = qg_ref[...] * (HEAD_DIM ** -0.5 * LOG2_E)
    q_refs = (q0_ref, q1_ref, q2_ref)
    heads_per_chunk = QZ_COL_CHUNK // HEAD_DIM
    for j in range(Q_WIDTH // QZ_COL_CHUNK):
        q = _dot(h, w_ref[:, j * QZ_COL_CHUNK:(j + 1) * QZ_COL_CHUNK])
        for hd in range(heads_per_chunk):
            head = j * heads_per_chunk + hd
            group, ghead = divmod(head, Q_HEADS_PER_GROUP)
            normed = _rms(q[:, hd * HEAD_DIM:(hd + 1) * HEAD_DIM], qg)
            if _dilations()[group] == 1:
                q_refs[group][0, 0, :, ghead * HEAD_DIM:(ghead + 1) * HEAD_DIM] = normed.astype(BF16)
            else:
                stage_ref[head] = normed
    for g in range(N_GROUPS):
        if _dilations()[g] > 1:
            heads = range(g * Q_HEADS_PER_GROUP, (g + 1) * Q_HEADS_PER_GROUP)
            _split_residues(stage_ref, heads, q_refs[g], _dilations()[g])
    for j in range(B_WIDTH // QZ_COL_CHUNK):
        cols = slice(Q_WIDTH + j * QZ_COL_CHUNK, Q_WIDTH + (j + 1) * QZ_COL_CHUNK)
        sz_ref[0, :, j * QZ_COL_CHUNK:(j + 1) * QZ_COL_CHUNK] = _silu(_dot(h, w_ref[:, cols])).astype(BF16)


def _qz_proj(x, mod, norm_g, w_in, q_norm_g):
    bsz, s, d = x.shape
    t = TOKEN_TILE
    return pl.pallas_call(
        _qz_kernel,
        grid=(bsz, s // t),
        in_specs=[
            pl.BlockSpec((1, t, d), lambda b, i: (b, i, 0)),
            pl.BlockSpec((1, 3, d), lambda b, i: (b, 0, 0)),
            _resident((1, d)),
            _resident((d, Q_WIDTH + B_WIDTH)),
            _resident((1, HEAD_DIM)),
        ],
        out_specs=[_dilated_block(t, dil, B_WIDTH) for dil in _dilations()]
        + [pl.BlockSpec((1, t, B_WIDTH), lambda b, i: (b, i, 0))],
        out_shape=[jax.ShapeDtypeStruct(_dilated_shape(bsz, s, dil, B_WIDTH), BF16) for dil in _dilations()]
        + [jax.ShapeDtypeStruct((bsz, s, B_WIDTH), BF16)],
        scratch_shapes=[pltpu.VMEM((N_Q_HEADS, t, HEAD_DIM), F32)],
        compiler_params=_params(),
        name="qz_proj",
    )(x, mod, norm_g, w_in, q_norm_g)


def _alibi_slope(head):
    return 2.0 ** (-8.0 * (head + 1) / N_Q_HEADS)


def _attn_kernel(q_ref, kp_ref, kc_ref, vp_ref, vc_ref, o_ref, lse_ref, bias_ref, *, group, n_res, n_bands):
    window, dil = DILATED_GROUPS[group]
    first = (pl.program_id(0) == 0) & (pl.program_id(1) == 0) & (pl.program_id(2) == 0)
    n = pl.program_id(2)

    @pl.when(first)
    def _():
        i = lax.broadcasted_iota(jnp.int32, (BAND, 2 * BAND), 0)
        j = lax.broadcasted_iota(jnp.int32, (BAND, 2 * BAND), 1)
        dq = BAND + i - j
        in_band = (dq >= 0) & (dq <= window // dil)
        dist = (dil * dq).astype(F32)
        for kv in range(KV_HEADS_PER_GROUP):
            for hq in range(Q_PER_KV):
                slope = _alibi_slope(group * Q_HEADS_PER_GROUP + kv * Q_PER_KV + hq)
                rows = slice(hq * BAND, (hq + 1) * BAND)
                bias = (-slope * LOG2_E) * dist
                bias_ref[1, kv, rows, :] = jnp.where(in_band, bias, MASKED)
                bias_ref[0, kv, rows, :] = jnp.where(in_band & (j >= BAND), bias, MASKED)

    has_prev = jnp.minimum(n, 1)
    lane = lax.broadcasted_iota(jnp.int32, (BAND, LANES), 1)
    for rr in range(n_res):
        for jb in range(n_bands):
            qrows = slice(jb * BAND, (jb + 1) * BAND)
            lse_tile = jnp.zeros((BAND, LANES), F32)
            for kv in range(KV_HEADS_PER_GROUP):
                kcols = slice(kv * HEAD_DIM, (kv + 1) * HEAD_DIM)
                if jb == 0:
                    k = jnp.concatenate([kp_ref[0, rr, :, kcols], kc_ref[0, rr, qrows, kcols]], axis=0)
                    v = jnp.concatenate([vp_ref[0, rr, :, kcols], vc_ref[0, rr, qrows, kcols]], axis=0)
                    bias = bias_ref[has_prev, kv]
                else:
                    krows = slice((jb - 1) * BAND, (jb + 1) * BAND)
                    k = kc_ref[0, rr, krows, kcols]
                    v = vc_ref[0, rr, krows, kcols]
                    bias = bias_ref[1, kv]
                q = jnp.concatenate(
                    [q_ref[0, rr, qrows, (kv * Q_PER_KV + hq) * HEAD_DIM:(kv * Q_PER_KV + hq + 1) * HEAD_DIM]
                     for hq in range(Q_PER_KV)], axis=0)
                s = lax.dot_general(q, k, (((1,), (1,)), ((), ())), preferred_element_type=F32) + bias
                m = jnp.max(s, axis=-1, keepdims=True)
                p = jnp.exp2(s - m)
                l = jnp.sum(p, axis=-1, keepdims=True)
                o = _dot(p.astype(BF16), v) / l
                lse = (m + jnp.log2(l)) * LN_2
                for hq in range(Q_PER_KV):
                    head = kv * Q_PER_KV + hq
                    rows = slice(hq * BAND, (hq + 1) * BAND)
                    o_ref[0, rr, qrows, head * HEAD_DIM:(head + 1) * HEAD_DIM] = o[rows].astype(BF16)
                    lse_tile = jnp.where(lane == head, lse[rows], lse_tile)
            lse_ref[0, rr, qrows, :] = lse_tile


ATTN_STEP = {1: (1, 4), 4: (1, 4), 16: (2, 2)}


def _dilated_attn(q, k, v, group):
    bsz, dil, sd, _ = q.shape
    nb = sd // BAND
    n_res, n_bands = ATTN_STEP[dil]
    assert dil % n_res == 0 and nb % n_bands == 0
    cur = lambda b, r, n: (b, r, n, 0)
    prev = lambda b, r, n: (b, r, jnp.maximum(n * n_bands - 1, 0), 0)
    kprev = pl.BlockSpec((1, n_res, BAND, KV_GROUP_WIDTH), prev)
    kcur = pl.BlockSpec((1, n_res, n_bands * BAND, KV_GROUP_WIDTH), cur)
    return pl.pallas_call(
        functools.partial(_attn_kernel, group=group, n_res=n_res, n_bands=n_bands),
        grid=(bsz, dil // n_res, nb // n_bands),
        in_specs=[pl.BlockSpec((1, n_res, n_bands * BAND, B_WIDTH), cur), kprev, kcur, kprev, kcur],
        out_specs=[pl.BlockSpec((1, n_res, n_bands * BAND, B_WIDTH), cur),
                   pl.BlockSpec((1, n_res, n_bands * BAND, LANES), cur)],
        out_shape=[jax.ShapeDtypeStruct(q.shape, BF16), jax.ShapeDtypeStruct((bsz, dil, sd, LANES), F32)],
        scratch_shapes=[pltpu.VMEM((2, KV_HEADS_PER_GROUP, Q_PER_KV * BAND, 2 * BAND), F32)],
        compiler_params=pltpu.CompilerParams(
            dimension_semantics=("arbitrary", "arbitrary", "arbitrary"),
            vmem_limit_bytes=VMEM_LIMIT_BYTES),
        name=f"dilated_attn_g{group}",
    )(q, k, k, v, v)


def _merge_kernel(x_ref, mod_ref, sz_ref, o0_ref, o1_ref, o2_ref, l0_ref, l1_ref, l2_ref, w_ref,
                  out_ref, y_ref, ostage_ref, lstage_ref):
    o_refs = (o0_ref, o1_ref, o2_ref)
    l_refs = (l0_ref, l1_ref, l2_ref)
    staged = [g for g in range(N_GROUPS) if _dilations()[g] > 1]
    for g in staged:
        dil = _dilations()[g]
        rows = o_refs[g].shape[2]
        for r in range(dil):
            dst = pl.ds(r, rows, stride=dil)
            for hd in range(Q_HEADS_PER_GROUP):
                ostage_ref[g, hd, dst, :] = o_refs[g][0, r, :, hd * HEAD_DIM:(hd + 1) * HEAD_DIM].astype(F32)
            lstage_ref[g, dst, :] = l_refs[g][0, r]

    def group_o(g, hd):
        if g in staged:
            return ostage_ref[g, hd]
        return o_refs[g][0, 0, :, hd * HEAD_DIM:(hd + 1) * HEAD_DIM].astype(F32)

    lses = [lstage_ref[g] if g in staged else l_refs[g][0, 0] for g in range(N_GROUPS)]
    top = jnp.maximum(jnp.maximum(lses[0], lses[1]), lses[2])
    es = [jnp.exp(l - top) for l in lses]
    den = es[0] + es[1] + es[2]
    alphas = [e / den for e in es]
    for hd in range(Q_HEADS_PER_GROUP):
        cols = slice(hd * HEAD_DIM, (hd + 1) * HEAD_DIM)
        o = alphas[0][:, hd:hd + 1] * group_o(0, hd)
        for g in range(1, N_GROUPS):
            o = o + alphas[g][:, hd:hd + 1] * group_o(g, hd)
        y_ref[:, cols] = (o * sz_ref[0, :, cols].astype(F32)).astype(BF16)
    out_ref[0] = x_ref[0] + mod_ref[0][2:3] * _dot(y_ref[...], w_ref[...])


def _merge_out(x, mod, sz, os_, lses, w_out):
    bsz, s, d = x.shape
    t = TOKEN_TILE
    tok = lambda w: pl.BlockSpec((1, t, w), lambda b, i: (b, i, 0))
    return pl.pallas_call(
        _merge_kernel,
        grid=(bsz, s // t),
        in_specs=[tok(d), pl.BlockSpec((1, 3, d), lambda b, i: (b, 0, 0)), tok(B_WIDTH)]
        + [_dilated_block(t, dil, B_WIDTH) for dil in _dilations()]
        + [_dilated_block(t, dil, LANES) for dil in _dilations()]
        + [_resident((B_WIDTH, d))],
        out_specs=tok(d),
        out_shape=jax.ShapeDtypeStruct(x.shape, x.dtype),
        scratch_shapes=[pltpu.VMEM((t, B_WIDTH), BF16),
                        pltpu.VMEM((N_GROUPS, Q_HEADS_PER_GROUP, t, HEAD_DIM), F32),
                        pltpu.VMEM((N_GROUPS, t, LANES), F32)],
        compiler_params=_params(),
        name="merge_out",
    )(x, mod, sz, *os_, *lses, w_out)


def kernel(x, c, a_ada_w, a_ada_b, a_norm_g, a_w_in, a_sgu_g, a_w_spatial, a_b_spatial, a_w_out,
           kv_ada_w, kv_ada_b, kv_norm_g, w_kv, k_norm_g, b_ada_w, b_ada_b, b_norm_g, b_w_in,
           b_q_norm_g, b_w_out):
    bsz = x.shape[0]
    d = D_MODEL
    assert x.shape[1] % TOKEN_TILE == 0 and all(TOKEN_TILE % dil == 0 for dil in _dilations())
    assert all(x.shape[1] % (dil * BAND) == 0 for dil in _dilations())

    a_mod = _ada(c, a_ada_w, a_ada_b).reshape(N_A_LAYERS, bsz, 3, d)
    kv_mod = _ada(c, kv_ada_w[None], kv_ada_b[None]).reshape(bsz, 2, d)
    b_mod = _ada(c, b_ada_w, b_ada_b).reshape(N_B_LAYERS, bsz, 3, d)

    a_w_in16 = a_w_in.astype(BF16)
    a_w_out16 = a_w_out.astype(BF16)
    b_s_t = jnp.swapaxes(a_b_spatial, 1, 2)
    for layer in range(N_A_LAYERS):
        x = _gmlp_layer(x, a_mod[layer], a_norm_g[layer][None], a_w_in16[layer], a_sgu_g[layer][None],
                        a_w_spatial[layer], b_s_t[layer], a_w_out16[layer])

    ks, vs = _shared_kv(x, kv_mod, kv_norm_g[None], w_kv.astype(BF16), k_norm_g[None])

    b_w_in16 = b_w_in.astype(BF16)
    b_w_out16 = b_w_out.astype(BF16)
    for layer in range(N_B_LAYERS):
        *qs, sz = _qz_proj(x, b_mod[layer], b_norm_g[layer][None], b_w_in16[layer], b_q_norm_g[layer][None])
        outs, lses = zip(*[_dilated_attn(qs[g], ks[g], vs[g], g) for g in range(N_GROUPS)])
        x = _merge_out(x, b_mod[layer], sz, outs, lses, b_w_out16[layer])
    return x
```
